```python
import math
import jax
import jax.numpy as jnp
from jax import lax
import numpy as np

D_MODEL = 1024
BATCH = 16
SEQ = 2048
DEPTH = 4

N_MIXERS = 2
N_ATTN_LAYERS = (DEPTH + 1) // 2
N_RET_LAYERS = DEPTH // 2

DA_HEAD_DIM = 64
DA_HEADS = D_MODEL // (2 * DA_HEAD_DIM)
DA_VALUE_DIM = 2 * DA_HEAD_DIM
ROT_DIM = DA_HEAD_DIM // 4
ROPE_THETA = 500000.0
Q_BLOCK = 128

RET_KEY_DIM = 256
RET_HEADS = D_MODEL // RET_KEY_DIM
RET_VALUE_DIM = 2 * RET_KEY_DIM
RET_CHUNK = 128
RET_THETA = 10000.0

FFN_DIM = 256 * ((8 * D_MODEL // 3 + 255) // 256)
CONV_WIDTH = 3
NORM_EPS = 1e-6

kernel_name = "hybrid_diffattn_retention_convffn"


def rms_norm(x, g):
    xf = x.astype(jnp.float32)
    y = xf * lax.rsqrt(jnp.mean(xf * xf, axis=-1, keepdims=True) + NORM_EPS)
    return (y * g.astype(jnp.float32)).astype(x.dtype)


def head_rms_norm(x):
    xf = x.astype(jnp.float32)
    return (xf * lax.rsqrt(jnp.mean(xf * xf, axis=-1, keepdims=True) + NORM_EPS)).astype(x.dtype)


def partial_rope(x, cos, sin):
    half = ROT_DIM // 2
    x1 = x[..., :half]
    x2 = x[..., half:ROT_DIM]
    return jnp.concatenate([x1 * cos - x2 * sin, x2 * cos + x1 * sin, x[..., ROT_DIM:]], axis=-1)


def diff_attention(h, positions, w_qkv, w_o, lq1, lk1, lq2, lk2, subln_g, lambda_init):
    B, S, _ = h.shape
    q, k, v = jnp.split(h @ w_qkv, 3, axis=-1)
    q = q.reshape(B, S, DA_HEADS, 2, DA_HEAD_DIM)
    k = k.reshape(B, S, DA_HEADS, 2, DA_HEAD_DIM)
    v = v.reshape(B, S, DA_HEADS, DA_VALUE_DIM)

    inv_freq = ROPE_THETA ** (-jnp.arange(0, ROT_DIM, 2, dtype=jnp.float32) / ROT_DIM)
    ang = positions.astype(jnp.float32)[..., None] * inv_freq
    cos = jnp.cos(ang)[:, :, None, None, :].astype(h.dtype)
    sin = jnp.sin(ang)[:, :, None, None, :].astype(h.dtype)
    q = partial_rope(q, cos, sin) * (DA_HEAD_DIM ** -0.5)
    k = partial_rope(k, cos, sin)

    lam = (jnp.exp(jnp.sum(lq1.astype(jnp.float32) * lk1.astype(jnp.float32)))
           - jnp.exp(jnp.sum(lq2.astype(jnp.float32) * lk2.astype(jnp.float32)))
           + lambda_init)

    nb = S // Q_BLOCK
    q_blocks = q.reshape(B, nb, Q_BLOCK, DA_HEADS, 2, DA_HEAD_DIM).transpose(1, 0, 2, 3, 4, 5)
    starts = jnp.arange(nb, dtype=jnp.int32) * Q_BLOCK
    key_idx = jnp.arange(S, dtype=jnp.int32)
    neg = jnp.finfo(jnp.float32).min

    def block(args):
        qb, start = args
        s = jnp.einsum('bqhcd,bkhcd->bhcqk', qb, k).astype(jnp.float32)
        causal = key_idx[None, :] <= (start + jnp.arange(Q_BLOCK, dtype=jnp.int32))[:, None]
        s = jnp.where(causal, s, neg)
        p = jax.nn.softmax(s, axis=-1)
        a = p[:, :, 0] - lam * p[:, :, 1]
        return jnp.einsum('bhqk,bkhe->bqhe', a.astype(v.dtype), v)

    o = lax.map(block, (q_blocks, starts))
    o = o.transpose(1, 0, 2, 3, 4).reshape(B, S, DA_HEADS, DA_VALUE_DIM)
    o = rms_norm(o, subln_g) * (1.0 - lambda_init)
    return o.reshape(B, S, DA_HEADS * DA_VALUE_DIM) @ w_o


def retention(h, positions, w_qkvg, w_o):
    B, S, _ = h.shape
    dk_all = RET_HEADS * RET_KEY_DIM
    dv_all = RET_HEADS * RET_VALUE_DIM
    q, k, v, g = jnp.split(h @ w_qkvg, [dk_all, 2 * dk_all, 2 * dk_all + dv_all], axis=-1)
    q = q.reshape(B, S, RET_HEADS, RET_KEY_DIM)
    k = k.reshape(B, S, RET_HEADS, RET_KEY_DIM)
    v = v.reshape(B, S, RET_HEADS, RET_VALUE_DIM)

    angle = 1.0 / (RET_THETA ** jnp.linspace(0.0, 1.0, RET_KEY_DIM // 2, dtype=jnp.float32))
    ang = positions.astype(jnp.float32)[..., None] * angle
    cos = jnp.cos(ang)[:, :, None, :].astype(h.dtype)
    sin = jnp.sin(ang)[:, :, None, :].astype(h.dtype)

    def rotate(t):
        te = t[..., 0::2]
        to = t[..., 1::2]
        return jnp.stack([te * cos - to * sin, to * cos + te * sin], axis=-1).reshape(t.shape)

    q = rotate(q)
    k = rotate(k) * (RET_KEY_DIM ** -0.5)

    log_gamma = jnp.log1p(-(2.0 ** (-5.0 - jnp.arange(RET_HEADS, dtype=jnp.float32))))
    C = RET_CHUNK
    idx = jnp.arange(C, dtype=jnp.float32)
    rel = idx[:, None] - idx[None, :]
    inner_decay = jnp.where(rel[None] >= 0,
                            jnp.exp(jnp.maximum(rel, 0.0)[None] * log_gamma[:, None, None]), 0.0)
    q_decay = jnp.exp((idx + 1.0)[:, None] * log_gamma[None, :])
    k_decay = jnp.exp((C - 1.0 - idx)[:, None] * log_gamma[None, :])
    chunk_decay = jnp.exp(C * log_gamma)

    nc = S // C
    qc = q.reshape(B, nc, C, RET_HEADS, RET_KEY_DIM).transpose(1, 0, 2, 3, 4)
    kc = k.reshape(B, nc, C, RET_HEADS, RET_KEY_DIM).transpose(1, 0, 2, 3, 4)
    vc = v.reshape(B, nc, C, RET_HEADS, RET_VALUE_DIM).transpose(1, 0, 2, 3, 4)

    def step(state, xs):
        qb, kb, vb = xs
        inner = jnp.einsum('bihd,bjhd->bhij', qb, kb) * inner_decay
        o_inner = jnp.einsum('bhij,bjhe->bihe', inner, vb)
        o_cross = jnp.einsum('bihd,bhde->bihe', qb * q_decay[..., None], state)
        state = (state * chunk_decay[None, :, None, None]
                 + jnp.einsum('bjhd,bjhe->bhde', kb * k_decay[..., None], vb))
        return state, o_inner + o_cross

    state0 = jnp.zeros((B, RET_HEADS, RET_KEY_DIM, RET_VALUE_DIM), jnp.float32)
    _, o = lax.scan(step, state0, (qc, kc, vc))
    o = o.transpose(1, 0, 2, 3, 4).reshape(B, S, RET_HEADS, RET_VALUE_DIM)
    o = head_rms_norm(o).reshape(B, S, dv_all)
    return (jax.nn.silu(g) * o) @ w_o


def conv_ffn(h, w_in, conv_w, conv_b, w_out):
    S = h.shape[1]
    gate, up = jnp.split(h @ w_in, 2, axis=-1)
    gp = jnp.pad(gate, ((0, 0), (CONV_WIDTH - 1, 0), (0, 0)))
    gate = conv_b + sum(gp[:, j:j + S] * conv_w[j] for j in range(CONV_WIDTH))
    return (jax.nn.silu(gate) * up) @ w_out


def setup_inputs(seed: int = 0) -> dict:
    key = jax.random.key(seed)
    ks = jax.random.split(key, 20)
    f32 = jnp.float32
    D = D_MODEL
    nrm = lambda k, shape, scale: jax.random.normal(k, shape, f32) * scale
    x = jax.random.normal(ks[0], (BATCH, SEQ, D), f32)
    offsets = jax.random.randint(ks[1], (BATCH, 1), 0, 4096, dtype=jnp.int32)
    positions = offsets + jnp.arange(SEQ, dtype=jnp.int32)[None, :]
    return {
        "x": x,
        "positions": positions,
        "norm_mix_g": 1.0 + nrm(ks[2], (DEPTH, D), 0.02),
        "norm_ffn_g": 1.0 + nrm(ks[3], (DEPTH, D), 0.02),
        "final_norm_g": 1.0 + nrm(ks[4], (D,), 0.02),
        "attn_w_qkv": nrm(ks[5], (N_ATTN_LAYERS, D, 3 * D), D ** -0.5),
        "attn_w_o": nrm(ks[6], (N_ATTN_LAYERS, DA_HEADS * DA_VALUE_DIM, D), (DA_HEADS * DA_VALUE_DIM) ** -0.5),
        "attn_lambda_q1": nrm(ks[7], (N_ATTN_LAYERS, DA_HEAD_DIM), 0.1),
        "attn_lambda_k1": nrm(ks[8], (N_ATTN_LAYERS, DA_HEAD_DIM), 0.1),
        "attn_lambda_q2": nrm(ks[9], (N_ATTN_LAYERS, DA_HEAD_DIM), 0.1),
        "attn_lambda_k2": nrm(ks[10], (N_ATTN_LAYERS, DA_HEAD_DIM), 0.1),
        "attn_subln_g": 1.0 + nrm(ks[11], (N_ATTN_LAYERS, DA_VALUE_DIM), 0.02),
        "ret_w_qkvg": nrm(ks[12], (N_RET_LAYERS, D, 2 * RET_HEADS * RET_KEY_DIM + 2 * RET_HEADS * RET_VALUE_DIM), D ** -0.5),
        "ret_w_o": nrm(ks[13], (N_RET_LAYERS, RET_HEADS * RET_VALUE_DIM, D), (RET_HEADS * RET_VALUE_DIM) ** -0.5),
        "ffn_w_in": nrm(ks[14], (DEPTH, D, 2 * FFN_DIM), D ** -0.5),
        "ffn_conv_w": nrm(ks[15], (DEPTH, CONV_WIDTH, FFN_DIM), CONV_WIDTH ** -0.5),
        "ffn_conv_b": nrm(ks[16], (DEPTH, FFN_DIM), 0.01),
        "ffn_w_out": nrm(ks[17], (DEPTH, FFN_DIM, D), FFN_DIM ** -0.5),
    }


def reference(x, positions, norm_mix_g, norm_ffn_g, final_norm_g, attn_w_qkv, attn_w_o,
              attn_lambda_q1, attn_lambda_k1, attn_lambda_q2, attn_lambda_k2, attn_subln_g,
              ret_w_qkvg, ret_w_o, ffn_w_in, ffn_conv_w, ffn_conv_b, ffn_w_out):
    for i in range(DEPTH):
        h = rms_norm(x, norm_mix_g[i])
        j = i // N_MIXERS
        if i % N_MIXERS == 0:
            lambda_init = 0.8 - 0.6 * math.exp(-0.3 * i)
            x = x + diff_attention(h, positions, attn_w_qkv[j], attn_w_o[j],
                                   attn_lambda_q1[j], attn_lambda_k1[j],
                                   attn_lambda_q2[j], attn_lambda_k2[j],
                                   attn_subln_g[j], lambda_init)
        else:
            x = x + retention(h, positions, ret_w_qkvg[j], ret_w_o[j])
        x = x + conv_ffn(rms_norm(x, norm_ffn_g[i]), ffn_w_in[i], ffn_conv_w[i], ffn_conv_b[i], ffn_w_out[i])
    return rms_norm(x, final_norm_g)
```

```python
import functools
import math

import numpy as np
import jax
import jax.numpy as jnp
from jax import lax
from jax.experimental import pallas as pl
from jax.experimental.pallas import tpu as pltpu

F32 = jnp.float32
BF16 = jnp.bfloat16

LANES = 128
SUBLANES = 8
VMEM_LIMIT_BYTES = 56 * 1024 * 1024

N_MIXERS = 2
DA_HEAD_DIM = 64
DA_VALUE_DIM = 2 * DA_HEAD_DIM
ROT_DIM = DA_HEAD_DIM // 4
ROPE_THETA = 500000.0
RET_KEY_DIM = 256
RET_VALUE_DIM = 2 * RET_KEY_DIM
RET_THETA = 10000.0
CONV_WIDTH = 3
NORM_EPS = 1e-6
MASK_VALUE = -1e30


def _dot(a, b):
    return jnp.dot(a, b, preferred_element_type=F32)


def _dot_nt(a, b):
    return lax.dot_general(a, b, (((1,), (1,)), ((), ())), preferred_element_type=F32)


def _dot_tn(a, b):
    return lax.dot_general(a, b, (((0,), (0,)), ((), ())), preferred_element_type=F32)


def _rms(x, g):
    ms = jnp.mean(x * x, axis=-1, keepdims=True)
    return x * lax.rsqrt(ms + NORM_EPS) * g


def _resident(shape):
    nd = len(shape)
    return pl.BlockSpec(shape, lambda *_: (0,) * nd, pipeline_mode=pl.Buffered(1))


def _params(*sem):
    return pltpu.CompilerParams(dimension_semantics=sem, vmem_limit_bytes=VMEM_LIMIT_BYTES)


def _attn_qkv_kernel(x_ref, g_ref, w_ref, c_ref, s1_ref, s2_ref, q_ref, k_ref, v_ref):
    d = x_ref.shape[1]
    h = _rms(x_ref[...], g_ref[...]).astype(BF16)
    c, s1, s2 = c_ref[...], s1_ref[...], s2_ref[...]
    half = ROT_DIM // 2

    def rope(y, c, s1, s2):
        return y * c + pltpu.roll(y, LANES - half, 1) * s1 + pltpu.roll(y, half, 1) * s2

    qs = DA_HEAD_DIM ** -0.5
    cq, s1q, s2q = c * qs, s1 * qs, s2 * qs
    yq = _dot(h, w_ref[:, 0:d])
    for j in range(d // LANES):
        sl = slice(j * LANES, (j + 1) * LANES)
        q_ref[:, sl] = rope(yq[:, sl], cq, s1q, s2q).astype(BF16)
    yk = _dot(h, w_ref[:, d:2 * d])
    for j in range(d // LANES):
        sl = slice(j * LANES, (j + 1) * LANES)
        k_ref[:, sl] = rope(yk[:, sl], c, s1, s2).astype(BF16)
    v_ref[...] = _dot(h, w_ref[:, 2 * d:3 * d]).astype(BF16)


def _attn_qkv(x2, g, w, tables, tm):
    m, d = x2.shape
    row = lambda i: (i, 0)
    act = jax.ShapeDtypeStruct((m, d), BF16)
    return pl.pallas_call(
        _attn_qkv_kernel,
        out_shape=(act, act, act),
        grid=(m // tm,),
        in_specs=[pl.BlockSpec((tm, d), row), _resident((1, d)), _resident((d, 3 * d)),
                  pl.BlockSpec((tm, LANES), row), pl.BlockSpec((tm, LANES), row),
                  pl.BlockSpec((tm, LANES), row)],
        out_specs=(pl.BlockSpec((tm, d), row),) * 3,
        compiler_params=_params("parallel"),
        name="attn_qkv",
    )(x2, g, w, *tables)


def _attn_rope_tables(positions):
    b, s = positions.shape
    half = ROT_DIM // 2
    inv_freq = ROPE_THETA ** (-jnp.arange(0, ROT_DIM, 2, dtype=F32) / ROT_DIM)
    ang = positions.astype(F32)[..., None] * inv_freq
    cos, sin = jnp.cos(ang), jnp.sin(ang)
    rest = DA_HEAD_DIM - ROT_DIM
    ones_r = jnp.ones((b, s, rest), F32)
    zeros_r = jnp.zeros((b, s, rest), F32)
    zeros_h = jnp.zeros((b, s, half), F32)
    c = jnp.concatenate([cos, cos, ones_r], axis=-1)
    s1 = jnp.concatenate([-sin, zeros_h, zeros_r], axis=-1)
    s2 = jnp.concatenate([zeros_h, sin, zeros_r], axis=-1)
    rep = LANES // DA_HEAD_DIM
    return tuple(jnp.tile(t, (1, 1, rep)).reshape(b * s, LANES) for t in (c, s1, s2))


def _diff_attn_kernel(lam_ref, sg_ref, q_ref, k_ref, v_ref, o_ref, *, tq, lambda_init):
    i = pl.program_id(2)
    lv = lam_ref[...]
    lam = (jnp.exp(jnp.sum(lv[0:1] * lv[1:2], axis=-1, keepdims=True))
           - jnp.exp(jnp.sum(lv[2:3] * lv[3:4], axis=-1, keepdims=True)) + lambda_init)

    q = q_ref[...]
    lane = lax.broadcasted_iota(jnp.int32, q.shape, 1)
    zero = jnp.zeros_like(q)
    qs = jnp.concatenate([jnp.where(lane < DA_HEAD_DIM, q, zero),
                          jnp.where(lane >= DA_HEAD_DIM, q, zero)], axis=0)

    def step(j, carry, masked):
        m, l, acc = carry
        start = pl.multiple_of(j * tq, tq)
        k = k_ref[pl.ds(start, tq), :]
        v = v_ref[pl.ds(start, tq), :]
        s = _dot_nt(qs, k)
        if masked:
            r = lax.broadcasted_iota(jnp.int32, s.shape, 0)
            r = jnp.where(r >= tq, r - tq, r)
            col = lax.broadcasted_iota(jnp.int32, s.shape, 1)
            s = jnp.where(col <= r, s, MASK_VALUE)
        m_new = jnp.maximum(m, jnp.max(s, axis=-1, keepdims=True))
        alpha = jnp.exp(m - m_new)
        p = jnp.exp(s - m_new)
        l = alpha * l + jnp.sum(p, axis=-1, keepdims=True)
        acc = alpha * acc + _dot(p.astype(BF16), v)
        return m_new, l, acc

    init = (jnp.full((2 * tq, 1), MASK_VALUE, F32), jnp.zeros((2 * tq, 1), F32),
            jnp.zeros((2 * tq, DA_VALUE_DIM), F32))
    carry = lax.fori_loop(0, i, functools.partial(step, masked=False), init)
    _, l, acc = step(i, carry, masked=True)
    o = acc * (1.0 / l)
    o = o[:tq] - lam * o[tq:]
    o_ref[...] = (_rms(o, sg_ref[...]) * (1.0 - lambda_init)).astype(o_ref.dtype)


def _diff_attn(lam_vecs, subln_g, q, k, v, b, s, tq, lambda_init):
    m, d = q.shape
    heads = d // DA_VALUE_DIM
    nq = s // tq
    qmap = lambda bi, hi, i: (bi * nq + i, hi)
    kvmap = lambda bi, hi, i: (bi, hi)
    return pl.pallas_call(
        functools.partial(_diff_attn_kernel, tq=tq, lambda_init=lambda_init),
        out_shape=jax.ShapeDtypeStruct((m, d), BF16),
        grid=(b, heads, nq),
        in_specs=[_resident(lam_vecs.shape), _resident(subln_g.shape),
                  pl.BlockSpec((tq, DA_VALUE_DIM), qmap),
                  pl.BlockSpec((s, DA_VALUE_DIM), kvmap),
                  pl.BlockSpec((s, DA_VALUE_DIM), kvmap)],
        out_specs=pl.BlockSpec((tq, DA_VALUE_DIM), qmap),
        compiler_params=_params("parallel", "parallel", "arbitrary"),
        name="diff_attn",
    )(lam_vecs, subln_g, q, k, v)


def _proj_residual_kernel(a_ref, w_ref, x_ref, o_ref):
    o_ref[...] = x_ref[...] + _dot(a_ref[...], w_ref[...])


def _proj_residual(a, w, x2, tm):
    m, d = x2.shape
    ka = a.shape[1]
    row = lambda i: (i, 0)
    return pl.pallas_call(
        _proj_residual_kernel,
        out_shape=jax.ShapeDtypeStruct((m, d), F32),
        grid=(m // tm,),
        in_specs=[pl.BlockSpec((tm, ka), row), _resident((ka, d)), pl.BlockSpec((tm, d), row)],
        out_specs=pl.BlockSpec((tm, d), row),
        compiler_params=_params("parallel"),
        name="proj_residual",
    )(a, w, x2)


def _ret_qkvg_kernel(x_ref, g_ref, w_ref, cos_ref, sin_ref, q_ref, k_ref, v_ref, gate_ref):
    d = x_ref.shape[1]
    dv = v_ref.shape[1]
    h = _rms(x_ref[...], g_ref[...]).astype(BF16)
    cos, sin = cos_ref[...], sin_ref[...]
    ks = RET_KEY_DIM ** -0.5
    cos_k, sin_k = cos * ks, sin * ks
    hk = RET_KEY_DIM // 2

    def rotate(y, out_ref, c, s):
        for j in range(d // RET_KEY_DIM):
            te = y[:, j * RET_KEY_DIM:j * RET_KEY_DIM + hk]
            to = y[:, j * RET_KEY_DIM + hk:(j + 1) * RET_KEY_DIM]
            out_ref[:, j * RET_KEY_DIM:j * RET_KEY_DIM + hk] = (te * c - to * s).astype(BF16)
            out_ref[:, j * RET_KEY_DIM + hk:(j + 1) * RET_KEY_DIM] = (to * c + te * s).astype(BF16)

    rotate(_dot(h, w_ref[:, 0:d]), q_ref, cos, sin)
    rotate(_dot(h, w_ref[:, d:2 * d]), k_ref, cos_k, sin_k)
    v_ref[...] = _dot(h, w_ref[:, 2 * d:2 * d + dv]).astype(BF16)
    gate_ref[...] = _dot(h, w_ref[:, 2 * d + dv:2 * d + 2 * dv]).astype(BF16)


def _ret_qkvg(x2, g, w, tables, tm):
    m, d = x2.shape
    dv = (w.shape[1] - 2 * d) // 2
    row = lambda i: (i, 0)
    hk = RET_KEY_DIM // 2
    return pl.pallas_call(
        _ret_qkvg_kernel,
        out_shape=(jax.ShapeDtypeStruct((m, d), BF16), jax.ShapeDtypeStruct((m, d), BF16),
                   jax.ShapeDtypeStruct((m, dv), BF16), jax.ShapeDtypeStruct((m, dv), BF16)),
        grid=(m // tm,),
        in_specs=[pl.BlockSpec((tm, d), row), _resident((1, d)), _resident(w.shape),
                  pl.BlockSpec((tm, hk), row), pl.BlockSpec((tm, hk), row)],
        out_specs=(pl.BlockSpec((tm, d), row), pl.BlockSpec((tm, d), row),
                   pl.BlockSpec((tm, dv), row), pl.BlockSpec((tm, dv), row)),
        compiler_params=_params("parallel"),
        name="ret_qkvg",
    )(x2, g, w, *tables)


def _ret_rope_tables(positions):
    b, s = positions.shape
    angle = 1.0 / (RET_THETA ** jnp.linspace(0.0, 1.0, RET_KEY_DIM // 2, dtype=F32))
    ang = positions.astype(F32)[..., None] * angle
    return (jnp.cos(ang).reshape(b * s, -1), jnp.sin(ang).reshape(b * s, -1))


def _ret_column_permutation(d):
    j = np.arange(RET_KEY_DIM // 2)
    head = np.concatenate([2 * j, 2 * j + 1])
    return (np.arange(d // RET_KEY_DIM)[:, None] * RET_KEY_DIM + head[None, :]).reshape(-1)


def _ret_decay_tables(heads, c):
    log_gamma = np.log1p(-(2.0 ** (-5.0 - np.arange(heads, dtype=np.float64))))
    idx = np.arange(c, dtype=np.float64)
    rel = idx[:, None] - idx[None, :]
    inner = np.where(rel[None] >= 0, np.exp(np.maximum(rel, 0.0)[None] * log_gamma[:, None, None]), 0.0)
    qd = np.exp((idx + 1.0)[None, :] * log_gamma[:, None])
    kd = np.exp((c - 1.0 - idx)[None, :] * log_gamma[:, None])
    cd = np.exp(c * log_gamma)
    qd = np.broadcast_to(qd[:, :, None], (heads, c, LANES))
    kd = np.broadcast_to(kd[:, :, None], (heads, c, LANES))
    cd = np.broadcast_to(cd[:, None, None], (heads, SUBLANES, LANES))
    as32 = lambda a: jnp.asarray(np.ascontiguousarray(a), dtype=F32)
    return as32(inner), as32(qd), as32(kd), as32(cd)


def _retention_kernel(inner_ref, qd_ref, kd_ref, cd_ref, q_ref, k_ref, v_ref, gate_ref, o_ref,
                      state_ref, *, c):
    s_len = q_ref.shape[0]
    inner_decay = inner_ref[0]
    qd = qd_ref[0][:, 0:1]
    kd = kd_ref[0][:, 0:1]
    cd = cd_ref[0][0:1, 0:1]
    state_ref[...] = jnp.zeros_like(state_ref)

    def chunk(t, _):
        rows = pl.ds(pl.multiple_of(t * c, c), c)
        q = q_ref[rows, :]
        k = k_ref[rows, :]
        v = v_ref[rows, :]
        state = state_ref[...]
        inner = _dot_nt(q, k) * inner_decay
        o = _dot(inner.astype(BF16), v)
        o = o + _dot((q.astype(F32) * qd).astype(BF16), state.astype(BF16))
        state_ref[...] = state * cd + _dot_tn((k.astype(F32) * kd).astype(BF16), v)
        o = o * lax.rsqrt(jnp.mean(o * o, axis=-1, keepdims=True) + NORM_EPS)
        g = gate_ref[rows, :].astype(F32)
        o_ref[rows, :] = (g * jax.nn.sigmoid(g) * o).astype(o_ref.dtype)
        return 0

    lax.fori_loop(0, s_len // c, chunk, 0)


def _retention(tables, q, k, v, gate, b, s, c):
    m, dk_all = q.shape
    dv_all = v.shape[1]
    heads = dk_all // RET_KEY_DIM
    seq = lambda bi, hi: (bi, hi)
    head = lambda bi, hi: (hi, 0, 0)
    inner, qd, kd, cd = tables
    return pl.pallas_call(
        functools.partial(_retention_kernel, c=c),
        out_shape=jax.ShapeDtypeStruct((m, dv_all), BF16),
        grid=(b, heads),
        in_specs=[pl.BlockSpec((1, c, c), head), pl.BlockSpec((1, c, LANES), head),
                  pl.BlockSpec((1, c, LANES), head), pl.BlockSpec((1, SUBLANES, LANES), head),
                  pl.BlockSpec((s, RET_KEY_DIM), seq), pl.BlockSpec((s, RET_KEY_DIM), seq),
                  pl.BlockSpec((s, RET_VALUE_DIM), seq), pl.BlockSpec((s, RET_VALUE_DIM), seq)],
        out_specs=pl.BlockSpec((s, RET_VALUE_DIM), seq),
        scratch_shapes=[pltpu.VMEM((RET_KEY_DIM, RET_VALUE_DIM), F32)],
        compiler_params=_params("parallel", "parallel"),
        name="retention",
    )(inner, qd, kd, cd, q, k, v, gate)


def _ffn_kernel(*refs, seq_len, tf, final):
    if final:
        x_ref, halo_ref, g_ref, w_in_ref, cw_ref, cb_ref, w_out_ref, fg_ref, o_ref, act_ref = refs
    else:
        x_ref, halo_ref, g_ref, w_in_ref, cw_ref, cb_ref, w_out_ref, o_ref, act_ref = refs
    tm = x_ref.shape[0]
    f = w_out_ref.shape[0]
    x = x_ref[...]
    g = g_ref[...]
    h = _rms(x, g).astype(BF16)
    hh = _rms(halo_ref[...], g).astype(BF16)
    seq_start = (pl.program_id(0) * tm) % seq_len == 0
    row = lax.broadcasted_iota(jnp.int32, (SUBLANES, tf), 0)

    for cidx in range(f // tf):
        cols = slice(cidx * tf, (cidx + 1) * tf)
        wg = w_in_ref[:, cols]
        gt = _dot(h, wg)
        up = _dot(h, w_in_ref[:, f + cidx * tf:f + (cidx + 1) * tf])
        gh = jnp.where(seq_start, 0.0, _dot(hh, wg))
        g1 = pltpu.roll(gt, 1, 0)
        g2 = pltpu.roll(gt, 2, 0)
        top1 = jnp.where(row < 1, pltpu.roll(gh, 1, 0), g1[:SUBLANES])
        top2 = jnp.where(row < 2, pltpu.roll(gh, 2, 0), g2[:SUBLANES])
        g1 = jnp.concatenate([top1, g1[SUBLANES:]], axis=0)
        g2 = jnp.concatenate([top2, g2[SUBLANES:]], axis=0)
        cw = cw_ref[:, cols]
        gate = cb_ref[:, cols] + g2 * cw[0:1] + g1 * cw[1:2] + gt * cw[2:3]
        act_ref[:, cols] = (gate * jax.nn.sigmoid(gate) * up).astype(BF16)

    y = x + _dot(act_ref[...], w_out_ref[...])
    if final:
        y = _rms(y, fg_ref[...])
    o_ref[...] = y


def _ffn(x2, g, w_in, conv_w, conv_b, w_out, final_g, seq_len, tm, tf):
    m, d = x2.shape
    f = w_out.shape[0]
    row = lambda i: (i, 0)
    halo = lambda i: (jnp.maximum(i * (tm // SUBLANES) - 1, 0), 0)
    final = final_g is not None
    in_specs = [pl.BlockSpec((tm, d), row), pl.BlockSpec((SUBLANES, d), halo), _resident((1, d)),
                _resident(w_in.shape), _resident(conv_w.shape), _resident(conv_b.shape),
                _resident(w_out.shape)]
    args = [x2, x2, g, w_in, conv_w, conv_b, w_out]
    if final:
        in_specs.append(_resident((1, d)))
        args.append(final_g)
    return pl.pallas_call(
        functools.partial(_ffn_kernel, seq_len=seq_len, tf=tf, final=final),
        out_shape=jax.ShapeDtypeStruct((m, d), F32),
        grid=(m // tm,),
        in_specs=in_specs,
        out_specs=pl.BlockSpec((tm, d), row),
        scratch_shapes=[pltpu.VMEM((tm, f), BF16)],
        compiler_params=_params("parallel"),
        name="conv_ffn",
    )(*args)


def _forward(x, positions, norm_mix_g, norm_ffn_g, final_norm_g, attn_w_qkv, attn_w_o,
             attn_lambda_q1, attn_lambda_k1, attn_lambda_q2, attn_lambda_k2, attn_subln_g,
             ret_w_qkvg, ret_w_o, ffn_w_in, ffn_conv_w, ffn_conv_b, ffn_w_out,
             *, tm, tq, ret_chunk, tf):
    b, s, d = x.shape
    depth = norm_mix_g.shape[0]
    x2 = x.reshape(b * s, d)
    attn_tables = _attn_rope_tables(positions)
    ret_tables = _ret_rope_tables(positions)
    ret_heads = d // RET_KEY_DIM
    decay_tables = _ret_decay_tables(ret_heads, ret_chunk)
    perm = _ret_column_permutation(d)

    for i in range(depth):
        j = i // N_MIXERS
        g_mix = norm_mix_g[i].reshape(1, d)
        if i % N_MIXERS == 0:
            lambda_init = 0.8 - 0.6 * math.exp(-0.3 * i)
            q, k, v = _attn_qkv(x2, g_mix, attn_w_qkv[j].astype(BF16), attn_tables, tm)
            lam_vecs = jnp.stack([attn_lambda_q1[j], attn_lambda_k1[j],
                                  attn_lambda_q2[j], attn_lambda_k2[j]]).astype(F32)
            o = _diff_attn(lam_vecs, attn_subln_g[j].reshape(1, -1), q, k, v, b, s, tq, lambda_init)
            x2 = _proj_residual(o, attn_w_o[j].astype(BF16), x2, tm)
        else:
            w = ret_w_qkvg[j]
            w = jnp.concatenate([w[:, :d][:, perm], w[:, d:2 * d][:, perm], w[:, 2 * d:]], axis=1)
            q, k, v, gate = _ret_qkvg(x2, g_mix, w.astype(BF16), ret_tables, tm)
            o = _retention(decay_tables, q, k, v, gate, b, s, ret_chunk)
            x2 = _proj_residual(o, ret_w_o[j].astype(BF16), x2, tm)
        final_g = final_norm_g.reshape(1, d) if i == depth - 1 else None
        x2 = _ffn(x2, norm_ffn_g[i].reshape(1, d), ffn_w_in[i].astype(BF16), ffn_conv_w[i],
                  ffn_conv_b[i].reshape(1, -1), ffn_w_out[i].astype(BF16), final_g, s, tm, tf)
    return x2.reshape(b, s, d)


def kernel(x, positions, norm_mix_g, norm_ffn_g, final_norm_g, attn_w_qkv, attn_w_o,
           attn_lambda_q1, attn_lambda_k1, attn_lambda_q2, attn_lambda_k2, attn_subln_g,
           ret_w_qkvg, ret_w_o, ffn_w_in, ffn_conv_w, ffn_conv_b, ffn_w_out):
    return _forward(x, positions, norm_mix_g, norm_ffn_g, final_norm_g, attn_w_qkv, attn_w_o,
                    attn_lambda_q1, attn_lambda_k1, attn_lambda_q2, attn_lambda_k2, attn_subln_g,
                    ret_w_qkvg, ret_w_o, ffn_w_in, ffn_conv_w, ffn_conv_b, ffn_w_out,
                    tm=512, tq=256, ret_chunk=128, tf=256)
```

```python
import functools
import math

import numpy as np
import jax
import jax.numpy as jnp
from jax import lax
from jax.experimental import pallas as pl
from jax.experimental.pallas import tpu as pltpu

F32 = jnp.float32
BF16 = jnp.bfloat16

LANES = 128
SUBLANES = 8
VMEM_LIMIT_BYTES = 56 * 1024 * 1024

N_MIXERS = 2
DA_HEAD_DIM = 64
DA_VALUE_DIM = 2 * DA_HEAD_DIM
ROT_DIM = DA_HEAD_DIM // 4
ROPE_THETA = 500000.0
RET_KEY_DIM = 256
RET_VALUE_DIM = 2 * RET_KEY_DIM
RET_THETA = 10000.0
CONV_WIDTH = 3
NORM_EPS = 1e-6
MASK_VALUE = -1e30
LOG2E = math.log2(math.e)


def _dot(a, b):
    return jnp.dot(a, b, preferred_element_type=F32)


def _dot_nt(a, b):
    return lax.dot_general(a, b, (((1,), (1,)), ((), ())), preferred_element_type=F32)


def _dot_tn(a, b):
    return lax.dot_general(a, b, (((0,), (0,)), ((), ())), preferred_element_type=F32)


def _rms(x, g):
    ms = jnp.mean(x * x, axis=-1, keepdims=True)
    return x * lax.rsqrt(ms + NORM_EPS) * g


def _resident(shape):
    nd = len(shape)
    return pl.BlockSpec(shape, lambda *_: (0,) * nd, pipeline_mode=pl.Buffered(1))


def _params(*sem):
    return pltpu.CompilerParams(dimension_semantics=sem, vmem_limit_bytes=VMEM_LIMIT_BYTES)


def _attn_qkv_kernel(x_ref, g_ref, w_ref, c_ref, s1_ref, s2_ref, ct_ref, st_ref,
                     qt_ref, k_ref, vt_ref):
    d = x_ref.shape[1]
    h = _rms(x_ref[...], g_ref[...]).astype(BF16)
    half = ROT_DIM // 2

    ct, st = ct_ref[...], st_ref[...]
    yq = _dot(h, w_ref[:, 0:d]) * (DA_HEAD_DIM ** -0.5 * LOG2E)
    for j in range(d // LANES):
        t = yq[:, j * LANES:(j + 1) * LANES].T
        parts = []
        for base in range(0, LANES, DA_HEAD_DIM):
            x1 = t[base:base + half]
            x2 = t[base + half:base + 2 * half]
            parts += [x1 * ct - x2 * st, x2 * ct + x1 * st, t[base + 2 * half:base + DA_HEAD_DIM]]
        qt_ref[j * LANES:(j + 1) * LANES, :] = jnp.concatenate(parts, axis=0).astype(BF16)

    c, s1, s2 = c_ref[...], s1_ref[...], s2_ref[...]
    yk = _dot(h, w_ref[:, d:2 * d])
    for j in range(d // LANES):
        sl = slice(j * LANES, (j + 1) * LANES)
        y = yk[:, sl]
        y = y * c + pltpu.roll(y, LANES - half, 1) * s1 + pltpu.roll(y, half, 1) * s2
        k_ref[:, sl] = y.astype(BF16)

    yv = _dot(h, w_ref[:, 2 * d:3 * d])
    for j in range(d // LANES):
        sl = slice(j * LANES, (j + 1) * LANES)
        vt_ref[sl, :] = yv[:, sl].T.astype(BF16)


def _attn_qkv(x2, g, w, tables, b, s, tm):
    m, d = x2.shape
    nblk = s // tm
    row = lambda i: (i, 0)
    tmap = lambda i: (i // nblk, 0, i % nblk)
    half = ROT_DIM // 2
    feat_major = jax.ShapeDtypeStruct((b, d, s), BF16)
    return pl.pallas_call(
        _attn_qkv_kernel,
        out_shape=(feat_major, jax.ShapeDtypeStruct((m, d), BF16), feat_major),
        grid=(m // tm,),
        in_specs=[pl.BlockSpec((tm, d), row), _resident((1, d)), _resident((d, 3 * d)),
                  pl.BlockSpec((tm, LANES), row), pl.BlockSpec((tm, LANES), row),
                  pl.BlockSpec((tm, LANES), row),
                  pl.BlockSpec((None, half, tm), tmap), pl.BlockSpec((None, half, tm), tmap)],
        out_specs=(pl.BlockSpec((None, d, tm), tmap), pl.BlockSpec((tm, d), row),
                   pl.BlockSpec((None, d, tm), tmap)),
        compiler_params=_params("parallel"),
        name="attn_qkv",
    )(x2, g, w, *tables)


def _attn_rope_tables(positions):
    b, s = positions.shape
    half = ROT_DIM // 2
    inv_freq = ROPE_THETA ** (-jnp.arange(0, ROT_DIM, 2, dtype=F32) / ROT_DIM)
    ang = positions.astype(F32)[..., None] * inv_freq
    cos, sin = jnp.cos(ang), jnp.sin(ang)
    rest = DA_HEAD_DIM - ROT_DIM
    ones_r = jnp.ones((b, s, rest), F32)
    zeros_r = jnp.zeros((b, s, rest), F32)
    zeros_h = jnp.zeros((b, s, half), F32)
    c = jnp.concatenate([cos, cos, ones_r], axis=-1)
    s1 = jnp.concatenate([-sin, zeros_h, zeros_r], axis=-1)
    s2 = jnp.concatenate([zeros_h, sin, zeros_r], axis=-1)
    rep = LANES // DA_HEAD_DIM
    token_major = tuple(jnp.tile(t, (1, 1, rep)).reshape(b * s, LANES) for t in (c, s1, s2))
    return token_major + (cos.transpose(0, 2, 1), sin.transpose(0, 2, 1))


def _diff_attn_kernel(lam_ref, sg_ref, qt_ref, k_ref, vt_ref, o_ref, *, tq, lambda_init):
    i = pl.program_id(2)
    lv = lam_ref[...]
    lam = (jnp.exp(jnp.sum(lv[0:1] * lv[1:2], axis=-1, keepdims=True))
           - jnp.exp(jnp.sum(lv[2:3] * lv[3:4], axis=-1, keepdims=True)) + lambda_init)

    qt = qt_ref[...]
    zero = jnp.zeros((DA_HEAD_DIM, tq), qt.dtype)
    rt = jnp.concatenate([jnp.concatenate([qt[:DA_HEAD_DIM], zero], axis=0),
                          jnp.concatenate([zero, qt[DA_HEAD_DIM:]], axis=0)], axis=1)

    def scores(j, masked):
        s = _dot(k_ref[j * tq:(j + 1) * tq, :], rt)
        if masked:
            key = lax.broadcasted_iota(jnp.int32, s.shape, 0)
            qi = lax.broadcasted_iota(jnp.int32, s.shape, 1)
            qi = jnp.where(qi >= tq, qi - tq, qi)
            s = jnp.where(key <= qi, s, MASK_VALUE)
        return s

    def update(j, s, carry):
        m, l, acc = carry
        m_new = jnp.maximum(m, jnp.max(s, axis=0, keepdims=True))
        alpha = jnp.exp2(m - m_new)
        p = jnp.exp2(s - m_new)
        l = alpha * l + jnp.sum(p, axis=0, keepdims=True)
        acc = alpha * acc + _dot(vt_ref[:, j * tq:(j + 1) * tq], p.astype(BF16))
        return m_new, l, acc

    init = (jnp.full((1, 2 * tq), MASK_VALUE, F32), jnp.zeros((1, 2 * tq), F32),
            jnp.zeros((DA_VALUE_DIM, 2 * tq), F32))

    def q_block(n_before):
        carry = init
        s = scores(0, masked=n_before == 0)
        for j in range(n_before + 1):
            s_next = scores(j + 1, masked=j + 1 == n_before) if j < n_before else None
            carry = update(j, s, carry)
            s = s_next
        _, l, acc = carry
        o = acc * (1.0 / l)
        o = o[:, :tq] - lam * o[:, tq:]
        o = o * lax.rsqrt(jnp.mean(o * o, axis=0, keepdims=True) + NORM_EPS)
        o_ref[...] = (o.T * (sg_ref[...] * (1.0 - lambda_init))).astype(o_ref.dtype)

    for ii in range(k_ref.shape[0] // tq):
        pl.when(i == ii)(functools.partial(q_block, ii))


def _diff_attn(lam_vecs, subln_g, qt, k, vt, b, s, tq, lambda_init):
    m, d = k.shape
    heads = d // DA_VALUE_DIM
    nq = s // tq
    omap = lambda bi, hi, i: (bi * nq + i, hi)
    return pl.pallas_call(
        functools.partial(_diff_attn_kernel, tq=tq, lambda_init=lambda_init),
        out_shape=jax.ShapeDtypeStruct((m, d), BF16),
        grid=(b, heads, nq),
        in_specs=[_resident(lam_vecs.shape), _resident(subln_g.shape),
                  pl.BlockSpec((None, DA_VALUE_DIM, tq), lambda bi, hi, i: (bi, hi, i)),
                  pl.BlockSpec((s, DA_VALUE_DIM), lambda bi, hi, i: (bi, hi)),
                  pl.BlockSpec((None, DA_VALUE_DIM, s), lambda bi, hi, i: (bi, hi, 0))],
        out_specs=pl.BlockSpec((tq, DA_VALUE_DIM), omap),
        compiler_params=_params("parallel", "parallel", "arbitrary"),
        name="diff_attn",
    )(lam_vecs, subln_g, qt, k, vt)


def _proj_residual_kernel(a_ref, w_ref, x_ref, o_ref):
    o_ref[...] = x_ref[...] + _dot(a_ref[...], w_ref[...])


def _proj_residual(a, w, x2, tm):
    m, d = x2.shape
    ka = a.shape[1]
    row = lambda i: (i, 0)
    return pl.pallas_call(
        _proj_residual_kernel,
        out_shape=jax.ShapeDtypeStruct((m, d), F32),
        grid=(m // tm,),
        in_specs=[pl.BlockSpec((tm, ka), row), _resident((ka, d)), pl.BlockSpec((tm, d), row)],
        out_specs=pl.BlockSpec((tm, d), row),
        compiler_params=_params("parallel"),
        name="proj_residual",
    )(a, w, x2)


def _ret_qkvg_kernel(x_ref, g_ref, w_ref, cos_ref, sin_ref, q_ref, k_ref, v_ref, gate_ref):
    d = x_ref.shape[1]
    dv = v_ref.shape[1]
    h = _rms(x_ref[...], g_ref[...]).astype(BF16)
    cos, sin = cos_ref[...], sin_ref[...]
    ks = RET_KEY_DIM ** -0.5
    cos_k, sin_k = cos * ks, sin * ks
    hk = RET_KEY_DIM // 2

    def rotate(y, out_ref, c, s):
        for j in range(d // RET_KEY_DIM):
            te = y[:, j * RET_KEY_DIM:j * RET_KEY_DIM + hk]
            to = y[:, j * RET_KEY_DIM + hk:(j + 1) * RET_KEY_DIM]
            out_ref[:, j * RET_KEY_DIM:j * RET_KEY_DIM + hk] = (te * c - to * s).astype(BF16)
            out_ref[:, j * RET_KEY_DIM + hk:(j + 1) * RET_KEY_DIM] = (to * c + te * s).astype(BF16)

    rotate(_dot(h, w_ref[:, 0:d]), q_ref, cos, sin)
    rotate(_dot(h, w_ref[:, d:2 * d]), k_ref, cos_k, sin_k)
    v_ref[...] = _dot(h, w_ref[:, 2 * d:2 * d + dv]).astype(BF16)
    gate_ref[...] = _dot(h, w_ref[:, 2 * d + dv:2 * d + 2 * dv]).astype(BF16)


def _ret_qkvg(x2, g, w, tables, tm):
    m, d = x2.shape
    dv = (w.shape[1] - 2 * d) // 2
    row = lambda i: (i, 0)
    hk = RET_KEY_DIM // 2
    return pl.pallas_call(
        _ret_qkvg_kernel,
        out_shape=(jax.ShapeDtypeStruct((m, d), BF16), jax.ShapeDtypeStruct((m, d), BF16),
                   jax.ShapeDtypeStruct((m, dv), BF16), jax.ShapeDtypeStruct((m, dv), BF16)),
        grid=(m // tm,),
        in_specs=[pl.BlockSpec((tm, d), row), _resident((1, d)), _resident(w.shape),
                  pl.BlockSpec((tm, hk), row), pl.BlockSpec((tm, hk), row)],
        out_specs=(pl.BlockSpec((tm, d), row), pl.BlockSpec((tm, d), row),
                   pl.BlockSpec((tm, dv), row), pl.BlockSpec((tm, dv), row)),
        compiler_params=_params("parallel"),
        name="ret_qkvg",
    )(x2, g, w, *tables)


def _ret_rope_tables(positions):
    b, s = positions.shape
    angle = 1.0 / (RET_THETA ** jnp.linspace(0.0, 1.0, RET_KEY_DIM // 2, dtype=F32))
    ang = positions.astype(F32)[..., None] * angle
    return (jnp.cos(ang).reshape(b * s, -1), jnp.sin(ang).reshape(b * s, -1))


def _ret_column_permutation(d):
    j = np.arange(RET_KEY_DIM // 2)
    head = np.concatenate([2 * j, 2 * j + 1])
    return (np.arange(d // RET_KEY_DIM)[:, None] * RET_KEY_DIM + head[None, :]).reshape(-1)


def _ret_decay_tables(heads, c):
    log_gamma = np.log1p(-(2.0 ** (-5.0 - np.arange(heads, dtype=np.float64))))
    idx = np.arange(c, dtype=np.float64)
    rel = idx[:, None] - idx[None, :]
    inner = np.where(rel[None] >= 0, np.exp(np.maximum(rel, 0.0)[None] * log_gamma[:, None, None]), 0.0)
    qd = np.exp((idx + 1.0)[None, :] * log_gamma[:, None])
    kd = np.exp((c - 1.0 - idx)[None, :] * log_gamma[:, None])
    cd = np.exp(c * log_gamma)
    qd = np.broadcast_to(qd[:, :, None], (heads, c, LANES))
    kd = np.broadcast_to(kd[:, :, None], (heads, c, LANES))
    cd = np.broadcast_to(cd[:, None, None], (heads, SUBLANES, LANES))
    as32 = lambda a: jnp.asarray(np.ascontiguousarray(a), dtype=F32)
    return as32(inner), as32(qd), as32(kd), as32(cd)


def _retention_kernel(inner_ref, qd_ref, kd_ref, cd_ref, q_ref, k_ref, v_ref, gate_ref, o_ref,
                      state_ref, *, c):
    s_len = q_ref.shape[0]
    inner_decay = inner_ref[0]
    qd = qd_ref[0][:, 0:1]
    kd = kd_ref[0][:, 0:1]
    cd = cd_ref[0][0:1, 0:1]
    state_ref[...] = jnp.zeros_like(state_ref)

    def chunk(t, _):
        rows = pl.ds(pl.multiple_of(t * c, c), c)
        q = q_ref[rows, :]
        k = k_ref[rows, :]
        v = v_ref[rows, :]
        state = state_ref[...]
        inner = _dot_nt(q, k) * inner_decay
        o = _dot(inner.astype(BF16), v)
        o = o + _dot((q.astype(F32) * qd).astype(BF16), state.astype(BF16))
        state_ref[...] = state * cd + _dot_tn((k.astype(F32) * kd).astype(BF16), v)
        o = o * lax.rsqrt(jnp.mean(o * o, axis=-1, keepdims=True) + NORM_EPS)
        g = gate_ref[rows, :].astype(F32)
        o_ref[rows, :] = (g * jax.nn.sigmoid(g) * o).astype(o_ref.dtype)
        return 0

    lax.fori_loop(0, s_len // c, chunk, 0)


def _retention(tables, q, k, v, gate, b, s, c):
    m, dk_all = q.shape
    dv_all = v.shape[1]
    heads = dk_all // RET_KEY_DIM
    seq = lambda bi, hi: (bi, hi)
    head = lambda bi, hi: (hi, 0, 0)
    inner, qd, kd, cd = tables
    return pl.pallas_call(
        functools.partial(_retention_kernel, c=c),
        out_shape=jax.ShapeDtypeStruct((m, dv_all), BF16),
        grid=(b, heads),
        in_specs=[pl.BlockSpec((1, c, c), head), pl.BlockSpec((1, c, LANES), head),
                  pl.BlockSpec((1, c, LANES), head), pl.BlockSpec((1, SUBLANES, LANES), head),
                  pl.BlockSpec((s, RET_KEY_DIM), seq), pl.BlockSpec((s, RET_KEY_DIM), seq),
                  pl.BlockSpec((s, RET_VALUE_DIM), seq), pl.BlockSpec((s, RET_VALUE_DIM), seq)],
        out_specs=pl.BlockSpec((s, RET_VALUE_DIM), seq),
        scratch_shapes=[pltpu.VMEM((RET_KEY_DIM, RET_VALUE_DIM), F32)],
        compiler_params=_params("parallel", "parallel"),
        name="retention",
    )(inner, qd, kd, cd, q, k, v, gate)


def _ffn_kernel(*refs, seq_len, tf, final):
    if final:
        x_ref, halo_ref, g_ref, w_in_ref, cw_ref, cb_ref, w_out_ref, fg_ref, o_ref, act_ref = refs
    else:
        x_ref, halo_ref, g_ref, w_in_ref, cw_ref, cb_ref, w_out_ref, o_ref, act_ref = refs
    tm = x_ref.shape[0]
    f = w_out_ref.shape[0]
    x = x_ref[...]
    g = g_ref[...]
    h = _rms(x, g).astype(BF16)
    hh = _rms(halo_ref[...], g).astype(BF16)
    seq_start = (pl.program_id(0) * tm) % seq_len == 0
    row = lax.broadcasted_iota(jnp.int32, (SUBLANES, tf), 0)

    for cidx in range(f // tf):
        cols = slice(cidx * tf, (cidx + 1) * tf)
        wg = w_in_ref[:, cols]
        gt = _dot(h, wg)
        up = _dot(h, w_in_ref[:, f + cidx * tf:f + (cidx + 1) * tf])
        gh = jnp.where(seq_start, 0.0, _dot(hh, wg))
        g1 = pltpu.roll(gt, 1, 0)
        g2 = pltpu.roll(gt, 2, 0)
        top1 = jnp.where(row < 1, pltpu.roll(gh, 1, 0), g1[:SUBLANES])
        top2 = jnp.where(row < 2, pltpu.roll(gh, 2, 0), g2[:SUBLANES])
        g1 = jnp.concatenate([top1, g1[SUBLANES:]], axis=0)
        g2 = jnp.concatenate([top2, g2[SUBLANES:]], axis=0)
        cw = cw_ref[:, cols]
        gate = cb_ref[:, cols] + g2 * cw[0:1] + g1 * cw[1:2] + gt * cw[2:3]
        act_ref[:, cols] = (gate * jax.nn.sigmoid(gate) * up).astype(BF16)

    y = x + _dot(act_ref[...], w_out_ref[...])
    if final:
        y = _rms(y, fg_ref[...])
    o_ref[...] = y


def _ffn(x2, g, w_in, conv_w, conv_b, w_out, final_g, seq_len, tm, tf):
    m, d = x2.shape
    f = w_out.shape[0]
    row = lambda i: (i, 0)
    halo = lambda i: (jnp.maximum(i * (tm // SUBLANES) - 1, 0), 0)
    final = final_g is not None
    in_specs = [pl.BlockSpec((tm, d), row), pl.BlockSpec((SUBLANES, d), halo), _resident((1, d)),
                _resident(w_in.shape), _resident(conv_w.shape), _resident(conv_b.shape),
                _resident(w_out.shape)]
    args = [x2, x2, g, w_in, conv_w, conv_b, w_out]
    if final:
        in_specs.append(_resident((1, d)))
        args.append(final_g)
    return pl.pallas_call(
        functools.partial(_ffn_kernel, seq_len=seq_len, tf=tf, final=final),
        out_shape=jax.ShapeDtypeStruct((m, d), F32),
        grid=(m // tm,),
        in_specs=in_specs,
        out_specs=pl.BlockSpec((tm, d), row),
        scratch_shapes=[pltpu.VMEM((tm, f), BF16)],
        compiler_params=_params("parallel"),
        name="conv_ffn",
    )(*args)


def _forward(x, positions, norm_mix_g, norm_ffn_g, final_norm_g, attn_w_qkv, attn_w_o,
             attn_lambda_q1, attn_lambda_k1, attn_lambda_q2, attn_lambda_k2, attn_subln_g,
             ret_w_qkvg, ret_w_o, ffn_w_in, ffn_conv_w, ffn_conv_b, ffn_w_out,
             *, tm, tq, ret_chunk, tf):
    b, s, d = x.shape
    depth = norm_mix_g.shape[0]
    x2 = x.reshape(b * s, d)
    attn_tables = _attn_rope_tables(positions)
    ret_tables = _ret_rope_tables(positions)
    ret_heads = d // RET_KEY_DIM
    decay_tables = _ret_decay_tables(ret_heads, ret_chunk)
    perm = _ret_column_permutation(d)

    for i in range(depth):
        j = i // N_MIXERS
        g_mix = norm_mix_g[i].reshape(1, d)
        if i % N_MIXERS == 0:
            lambda_init = 0.8 - 0.6 * math.exp(-0.3 * i)
            qt, k, vt = _attn_qkv(x2, g_mix, attn_w_qkv[j].astype(BF16), attn_tables, b, s, tm)
            lam_vecs = jnp.stack([attn_lambda_q1[j], attn_lambda_k1[j],
                                  attn_lambda_q2[j], attn_lambda_k2[j]]).astype(F32)
            o = _diff_attn(lam_vecs, attn_subln_g[j].reshape(1, -1), qt, k, vt, b, s, tq, lambda_init)
            x2 = _proj_residual(o, attn_w_o[j].astype(BF16), x2, tm)
        else:
            w = ret_w_qkvg[j]
            w = jnp.concatenate([w[:, :d][:, perm], w[:, d:2 * d][:, perm], w[:, 2 * d:]], axis=1)
            q, k, v, gate = _ret_qkvg(x2, g_mix, w.astype(BF16), ret_tables, tm)
            o = _retention(decay_tables, q, k, v, gate, b, s, ret_chunk)
            x2 = _proj_residual(o, ret_w_o[j].astype(BF16), x2, tm)
        final_g = final_norm_g.reshape(1, d) if i == depth - 1 else None
        x2 = _ffn(x2, norm_ffn_g[i].reshape(1, d), ffn_w_in[i].astype(BF16), ffn_conv_w[i],
                  ffn_conv_b[i].reshape(1, -1), ffn_w_out[i].astype(BF16), final_g, s, tm, tf)
    return x2.reshape(b, s, d)


def kernel(x, positions, norm_mix_g, norm_ffn_g, final_norm_g, attn_w_qkv, attn_w_o,
           attn_lambda_q1, attn_lambda_k1, attn_lambda_q2, attn_lambda_k2, attn_subln_g,
           ret_w_qkvg, ret_w_o, ffn_w_in, ffn_conv_w, ffn_conv_b, ffn_w_out):
    return _forward(x, positions, norm_mix_g, norm_ffn_g, final_norm_g, attn_w_qkv, attn_w_o,
                    attn_lambda_q1, attn_lambda_k1, attn_lambda_q2, attn_lambda_k2, attn_subln_g,
                    ret_w_qkvg, ret_w_o, ffn_w_in, ffn_conv_w, ffn_conv_b, ffn_w_out,
                    tm=512, tq=256, ret_chunk=128, tf=256)
```

```python
import functools
import math

import numpy as np
import jax
import jax.numpy as jnp
from jax import lax
from jax.experimental import pallas as pl
from jax.experimental.pallas import tpu as pltpu

F32 = jnp.float32
BF16 = jnp.bfloat16

LANES = 128
SUBLANES = 8
BF16_SUBLANES = 16
HALO = BF16_SUBLANES
VMEM_LIMIT_BYTES = 56 * 1024 * 1024

N_MIXERS = 2
DA_HEAD_DIM = 64
DA_VALUE_DIM = 2 * DA_HEAD_DIM
ROT_DIM = DA_HEAD_DIM // 4
ROPE_THETA = 500000.0
RET_KEY_DIM = 256
RET_VALUE_DIM = 2 * RET_KEY_DIM
RET_THETA = 10000.0
CONV_WIDTH = 3
NORM_EPS = 1e-6
MASK_VALUE = -1e30
LOG2E = math.log2(math.e)
CONCURRENT_Q_BLOCKS = 4


def _dot(a, b):
    return jnp.dot(a, b, preferred_element_type=F32)


def _dot_nt(a, b):
    return lax.dot_general(a, b, (((1,), (1,)), ((), ())), preferred_element_type=F32)


def _dot_tn(a, b):
    return lax.dot_general(a, b, (((0,), (0,)), ((), ())), preferred_element_type=F32)


def _rms(x, g):
    ms = jnp.mean(x * x, axis=-1, keepdims=True)
    return x * lax.rsqrt(ms + NORM_EPS) * g


def _resident(shape):
    nd = len(shape)
    return pl.BlockSpec(shape, lambda *_: (0,) * nd, pipeline_mode=pl.Buffered(1))


def _params(*sem):
    return pltpu.CompilerParams(dimension_semantics=sem, vmem_limit_bytes=VMEM_LIMIT_BYTES)


def _attn_qkv_kernel(x_ref, g_ref, w_ref, c_ref, s1_ref, s2_ref, ct_ref, st_ref,
                     qt_ref, k_ref, vt_ref):
    d = x_ref.shape[1]
    h = _rms(x_ref[...], g_ref[...]).astype(BF16)
    half = ROT_DIM // 2

    ct, st = ct_ref[...], st_ref[...]
    yq = _dot(h, w_ref[:, 0:d]) * (DA_HEAD_DIM ** -0.5 * LOG2E)
    for j in range(d // LANES):
        t = yq[:, j * LANES:(j + 1) * LANES].T
        parts = []
        for base in range(0, LANES, DA_HEAD_DIM):
            x1 = t[base:base + half]
            x2 = t[base + half:base + 2 * half]
            parts += [x1 * ct - x2 * st, x2 * ct + x1 * st, t[base + 2 * half:base + DA_HEAD_DIM]]
        qt_ref[j * LANES:(j + 1) * LANES, :] = jnp.concatenate(parts, axis=0).astype(BF16)

    c, s1, s2 = c_ref[...], s1_ref[...], s2_ref[...]
    yk = _dot(h, w_ref[:, d:2 * d])
    for j in range(d // LANES):
        sl = slice(j * LANES, (j + 1) * LANES)
        y = yk[:, sl]
        y = y * c + pltpu.roll(y, LANES - half, 1) * s1 + pltpu.roll(y, half, 1) * s2
        k_ref[:, sl] = y.astype(BF16)

    yv = _dot(h, w_ref[:, 2 * d:3 * d])
    for j in range(d // LANES):
        sl = slice(j * LANES, (j + 1) * LANES)
        vt_ref[sl, :] = yv[:, sl].T.astype(BF16)


def _attn_qkv(x2, g, w, tables, b, s, tm):
    m, d = x2.shape
    nblk = s // tm
    row = lambda i: (i, 0)
    tmap = lambda i: (i // nblk, 0, i % nblk)
    half = ROT_DIM // 2
    feat_major = jax.ShapeDtypeStruct((b, d, s), BF16)
    return pl.pallas_call(
        _attn_qkv_kernel,
        out_shape=(feat_major, jax.ShapeDtypeStruct((m, d), BF16), feat_major),
        grid=(m // tm,),
        in_specs=[pl.BlockSpec((tm, d), row), _resident((1, d)), _resident((d, 3 * d)),
                  pl.BlockSpec((tm, LANES), row), pl.BlockSpec((tm, LANES), row),
                  pl.BlockSpec((tm, LANES), row),
                  pl.BlockSpec((None, half, tm), tmap), pl.BlockSpec((None, half, tm), tmap)],
        out_specs=(pl.BlockSpec((None, d, tm), tmap), pl.BlockSpec((tm, d), row),
                   pl.BlockSpec((None, d, tm), tmap)),
        compiler_params=_params("parallel"),
        name="attn_qkv",
    )(x2, g, w, *tables)


def _attn_rope_tables(positions):
    b, s = positions.shape
    half = ROT_DIM // 2
    inv_freq = ROPE_THETA ** (-jnp.arange(0, ROT_DIM, 2, dtype=F32) / ROT_DIM)
    ang = positions.astype(F32)[..., None] * inv_freq
    cos, sin = jnp.cos(ang), jnp.sin(ang)
    rest = DA_HEAD_DIM - ROT_DIM
    ones_r = jnp.ones((b, s, rest), F32)
    zeros_r = jnp.zeros((b, s, rest), F32)
    zeros_h = jnp.zeros((b, s, half), F32)
    c = jnp.concatenate([cos, cos, ones_r], axis=-1)
    s1 = jnp.concatenate([-sin, zeros_h, zeros_r], axis=-1)
    s2 = jnp.concatenate([zeros_h, sin, zeros_r], axis=-1)
    rep = LANES // DA_HEAD_DIM
    token_major = tuple(jnp.tile(t, (1, 1, rep)).reshape(b * s, LANES) for t in (c, s1, s2))
    return token_major + (cos.transpose(0, 2, 1), sin.transpose(0, 2, 1))


def _diff_attn_kernel(lam_ref, sg_ref, qt_ref, k_ref, vt_ref, o_ref, *, tq, lambda_init):
    lv = lam_ref[...]
    lam = (jnp.exp(jnp.sum(lv[0:1] * lv[1:2], axis=-1, keepdims=True))
           - jnp.exp(jnp.sum(lv[2:3] * lv[3:4], axis=-1, keepdims=True)) + lambda_init)
    out_scale = sg_ref[...] * (1.0 - lambda_init)
    zero = jnp.zeros((DA_HEAD_DIM, tq), qt_ref.dtype)

    def scores(rt, j, masked):
        s = _dot(k_ref[j * tq:(j + 1) * tq, :], rt)
        if masked:
            key = lax.broadcasted_iota(jnp.int32, s.shape, 0)
            qi = lax.broadcasted_iota(jnp.int32, s.shape, 1)
            qi = jnp.where(qi >= tq, qi - tq, qi)
            s = jnp.where(key <= qi, s, MASK_VALUE)
        return s

    ones = jnp.ones((BF16_SUBLANES, tq), vt_ref.dtype)

    def update(j, s, carry):
        m, acc = carry
        m_new = jnp.maximum(m, jnp.max(s, axis=0, keepdims=True))
        alpha = jnp.exp2(m - m_new)
        p = jnp.exp2(s - m_new)
        vt1 = jnp.concatenate([vt_ref[:, j * tq:(j + 1) * tq], ones], axis=0)
        acc = alpha * acc + _dot(vt1, p.astype(BF16))
        return m_new, acc

    init = (jnp.full((1, 2 * tq), MASK_VALUE, F32),
            jnp.zeros((DA_VALUE_DIM + BF16_SUBLANES, 2 * tq), F32))

    def q_block(n_before):
        qt = qt_ref[:, n_before * tq:(n_before + 1) * tq]
        rt = jnp.concatenate([jnp.concatenate([qt[:DA_HEAD_DIM], zero], axis=0),
                              jnp.concatenate([zero, qt[DA_HEAD_DIM:]], axis=0)], axis=1)
        carry = init
        s = scores(rt, 0, masked=n_before == 0)
        yield
        for j in range(n_before + 1):
            s_next = scores(rt, j + 1, masked=j + 1 == n_before) if j < n_before else None
            yield
            carry = update(j, s, carry)
            s = s_next
            yield
        _, acc = carry
        o = acc[:DA_VALUE_DIM] * (1.0 / acc[DA_VALUE_DIM:DA_VALUE_DIM + 1])
        o = o[:, :tq] - lam * o[:, tq:]
        o = o * lax.rsqrt(jnp.mean(o * o, axis=0, keepdims=True) + NORM_EPS)
        o_ref[n_before * tq:(n_before + 1) * tq, :] = (o.T * out_scale).astype(o_ref.dtype)

    pending = [q_block(ii) for ii in reversed(range(k_ref.shape[0] // tq))]
    active = [pending.pop(0) for _ in range(min(CONCURRENT_Q_BLOCKS, len(pending)))]
    while active:
        for g in list(active):
            try:
                next(g)
            except StopIteration:
                active.remove(g)
                if pending:
                    active.append(pending.pop(0))


def _diff_attn(lam_vecs, subln_g, qt, k, vt, b, s, tq, lambda_init):
    m, d = k.shape
    heads = d // DA_VALUE_DIM
    feat = pl.BlockSpec((None, DA_VALUE_DIM, s), lambda bi, hi: (bi, hi, 0))
    tok = pl.BlockSpec((s, DA_VALUE_DIM), lambda bi, hi: (bi, hi))
    return pl.pallas_call(
        functools.partial(_diff_attn_kernel, tq=tq, lambda_init=lambda_init),
        out_shape=jax.ShapeDtypeStruct((m, d), BF16),
        grid=(b, heads),
        in_specs=[_resident(lam_vecs.shape), _resident(subln_g.shape), feat, tok, feat],
        out_specs=tok,
        compiler_params=_params("parallel", "parallel"),
        name="diff_attn",
    )(lam_vecs, subln_g, qt, k, vt)


def _gated_heads(o, gate):
    parts = []
    for j in range(o.shape[1] // RET_VALUE_DIM):
        sl = slice(j * RET_VALUE_DIM, (j + 1) * RET_VALUE_DIM)
        oj = o[:, sl].astype(F32)
        oj = oj * lax.rsqrt(jnp.mean(oj * oj, axis=-1, keepdims=True) + NORM_EPS)
        g = gate[:, sl].astype(F32)
        parts.append((g * jax.nn.sigmoid(g) * oj).astype(BF16))
    return jnp.concatenate(parts, axis=1)


def _ret_qkvg_kernel(x_ref, g_ref, w_ref, cos_ref, sin_ref, cost_ref, sint_ref, qd_ref, kd_ref,
                     q_ref, kt_ref, v_ref, gate_ref):
    d = x_ref.shape[1]
    dv = v_ref.shape[1]
    h = _rms(x_ref[...], g_ref[...]).astype(BF16)
    hk = RET_KEY_DIM // 2

    cos, sin = cos_ref[...], sin_ref[...]
    yq = _dot(h, w_ref[:, 0:d])
    for j in range(d // RET_KEY_DIM):
        qd = qd_ref[j]
        c, s = cos * qd, sin * qd
        te = yq[:, j * RET_KEY_DIM:j * RET_KEY_DIM + hk]
        to = yq[:, j * RET_KEY_DIM + hk:(j + 1) * RET_KEY_DIM]
        q_ref[:, j * RET_KEY_DIM:j * RET_KEY_DIM + hk] = (te * c - to * s).astype(BF16)
        q_ref[:, j * RET_KEY_DIM + hk:(j + 1) * RET_KEY_DIM] = (to * c + te * s).astype(BF16)

    cost, sint = cost_ref[...], sint_ref[...]
    yk = _dot(h, w_ref[:, d:2 * d])
    for j in range(d // RET_KEY_DIM):
        kd = kd_ref[j][0:1, :] * (RET_KEY_DIM ** -0.5)
        c, s = cost * kd, sint * kd
        te = yk[:, j * RET_KEY_DIM:j * RET_KEY_DIM + hk].T
        to = yk[:, j * RET_KEY_DIM + hk:(j + 1) * RET_KEY_DIM].T
        kt_ref[j * RET_KEY_DIM:j * RET_KEY_DIM + hk, :] = (te * c - to * s).astype(BF16)
        kt_ref[j * RET_KEY_DIM + hk:(j + 1) * RET_KEY_DIM, :] = (to * c + te * s).astype(BF16)

    v_ref[...] = _dot(h, w_ref[:, 2 * d:2 * d + dv]).astype(BF16)
    gate_ref[...] = _dot(h, w_ref[:, 2 * d + dv:2 * d + 2 * dv]).astype(BF16)


def _ret_qkvg(x2, g, w, tables, decay, b, s, tm):
    m, d = x2.shape
    dv = (w.shape[1] - 2 * d) // 2
    nblk = s // tm
    row = lambda i: (i, 0)
    tmap = lambda i: (i // nblk, 0, i % nblk)
    hk = RET_KEY_DIM // 2
    return pl.pallas_call(
        _ret_qkvg_kernel,
        out_shape=(jax.ShapeDtypeStruct((m, d), BF16), jax.ShapeDtypeStruct((b, d, s), BF16),
                   jax.ShapeDtypeStruct((m, dv), BF16), jax.ShapeDtypeStruct((m, dv), BF16)),
        grid=(m // tm,),
        in_specs=[pl.BlockSpec((tm, d), row), _resident((1, d)), _resident(w.shape),
                  pl.BlockSpec((tm, hk), row), pl.BlockSpec((tm, hk), row),
                  pl.BlockSpec((None, hk, tm), tmap), pl.BlockSpec((None, hk, tm), tmap),
                  _resident(decay[0].shape), _resident(decay[1].shape)],
        out_specs=(pl.BlockSpec((tm, d), row), pl.BlockSpec((None, d, tm), tmap),
                   pl.BlockSpec((tm, dv), row), pl.BlockSpec((tm, dv), row)),
        compiler_params=_params("parallel"),
        name="ret_qkvg",
    )(x2, g, w, *tables, *decay)


def _ret_rope_tables(positions):
    b, s = positions.shape
    angle = 1.0 / (RET_THETA ** jnp.linspace(0.0, 1.0, RET_KEY_DIM // 2, dtype=F32))
    ang = positions.astype(F32)[..., None] * angle
    cos, sin = jnp.cos(ang), jnp.sin(ang)
    return (cos.reshape(b * s, -1), sin.reshape(b * s, -1),
            cos.transpose(0, 2, 1), sin.transpose(0, 2, 1))


def _ret_column_permutation(d):
    j = np.arange(RET_KEY_DIM // 2)
    head = np.concatenate([2 * j, 2 * j + 1])
    return (np.arange(d // RET_KEY_DIM)[:, None] * RET_KEY_DIM + head[None, :]).reshape(-1)


def _ret_decay_tables(heads, c, tm):
    log_gamma = np.log1p(-(2.0 ** (-5.0 - np.arange(heads, dtype=np.float64))))
    idx = np.arange(c, dtype=np.float64)
    reps = tm // c
    qd = np.tile(np.exp((idx + 1.0)[None, :] * log_gamma[:, None]), (1, reps))
    kd = np.tile(np.exp((c - 1.0 - idx)[None, :] * log_gamma[:, None]), (1, reps))
    causal = idx[:, None] >= idx[None, :]
    mask = np.where(causal[None], np.exp(-c * log_gamma)[:, None, None], 0.0)
    cd = np.exp(c * log_gamma)
    qd = np.broadcast_to(qd[:, :, None], (heads, tm, LANES))
    kd = np.broadcast_to(kd[:, None, :], (heads, SUBLANES, tm))
    cd = np.broadcast_to(cd[:, None, None], (heads, SUBLANES, LANES))
    as32 = lambda a: jnp.asarray(np.ascontiguousarray(a), dtype=F32)
    return (as32(qd), as32(kd)), (as32(mask), as32(cd))


def _retention_kernel(mask_ref, cd_ref, q_ref, kt_ref, v_ref, o_ref, *, c):
    s_len = q_ref.shape[0]
    mask = mask_ref[0]
    cd = cd_ref[0][0:1, 0:1]

    state = None
    n_chunks = s_len // c
    rows = lambda t: slice(t * c, (t + 1) * c)
    scores = _dot(q_ref[rows(0), :], kt_ref[:, rows(0)])
    for t in range(n_chunks):
        q = q_ref[rows(t), :]
        v = v_ref[rows(t), :]
        cur = scores
        if t + 1 < n_chunks:
            scores = _dot(q_ref[rows(t + 1), :], kt_ref[:, rows(t + 1)])
            kv = _dot(kt_ref[:, rows(t)], v)
        inner = (cur * mask).astype(BF16)
        if state is None:
            o = _dot(inner, v)
        else:
            o = _dot(jnp.concatenate([inner, q], axis=1),
                     jnp.concatenate([v, state.astype(BF16)], axis=0))
        if t + 1 < n_chunks:
            state = kv if state is None else state * cd + kv
        o_ref[rows(t), :] = o.astype(o_ref.dtype)


def _retention(tables, q, kt, v, b, s, c):
    m, dk_all = q.shape
    dv_all = v.shape[1]
    heads = dk_all // RET_KEY_DIM
    seq = lambda bi, hi: (bi, hi)
    head = lambda bi, hi: (hi, 0, 0)
    mask, cd = tables
    return pl.pallas_call(
        functools.partial(_retention_kernel, c=c),
        out_shape=jax.ShapeDtypeStruct((m, dv_all), BF16),
        grid=(b, heads),
        in_specs=[pl.BlockSpec((1, c, c), head), pl.BlockSpec((1, SUBLANES, LANES), head),
                  pl.BlockSpec((s, RET_KEY_DIM), seq),
                  pl.BlockSpec((None, RET_KEY_DIM, s), lambda bi, hi: (bi, hi, 0)),
                  pl.BlockSpec((s, RET_VALUE_DIM), seq)],
        out_specs=pl.BlockSpec((s, RET_VALUE_DIM), seq),
        compiler_params=_params("parallel", "parallel"),
        name="retention",
    )(mask, cd, q, kt, v)


def _mixer_ffn_kernel(*refs, seq_len, tf, gated, final):
    n_mix = 2 if gated else 1
    x_ref, xh_ref = refs[0:2]
    mix_refs = refs[2:2 + 2 * n_mix]
    wo_ref, g_ref, w_in_ref, cw_ref, cb_ref, w_out_ref = refs[2 + 2 * n_mix:8 + 2 * n_mix]
    rest = refs[8 + 2 * n_mix:]
    fg_ref = rest[0] if final else None
    o_ref, act_ref = rest[-2:]
    tm = x_ref.shape[0]
    f = w_out_ref.shape[0]

    with_halo = lambda blk_ref, halo_ref: jnp.concatenate([halo_ref[...], blk_ref[...]], axis=0)
    mix = [with_halo(mix_refs[2 * n], mix_refs[2 * n + 1]) for n in range(n_mix)]
    a = _gated_heads(*mix) if gated else mix[0]
    x1 = with_halo(x_ref, xh_ref) + _dot(a, wo_ref[...])
    h = _rms(x1, g_ref[...]).astype(BF16)
    seq_start = (pl.program_id(0) * tm) % seq_len == 0

    for cidx in range(f // tf):
        cols = slice(cidx * tf, (cidx + 1) * tf)
        gt = _dot(h, w_in_ref[:, cols])
        up = _dot(h[HALO:], w_in_ref[:, f + cidx * tf:f + (cidx + 1) * tf])
        gt = jnp.concatenate([jnp.where(seq_start, 0.0, gt[:HALO]), gt[HALO:]], axis=0)
        g1 = pltpu.roll(gt, 1, 0)[HALO:]
        g2 = pltpu.roll(gt, 2, 0)[HALO:]
        cw = cw_ref[:, cols]
        gate = cb_ref[:, cols] + g2 * cw[0:1] + g1 * cw[1:2] + gt[HALO:] * cw[2:3]
        act_ref[:, cols] = (gate * jax.nn.sigmoid(gate) * up).astype(BF16)

    y = x1[HALO:] + _dot(act_ref[...], w_out_ref[...])
    if final:
        y = _rms(y, fg_ref[...])
    o_ref[...] = y


def _mixer_ffn(x2, mix, w_o, g, w_in, conv_w, conv_b, w_out, final_g, seq_len, tm, tf):
    m, d = x2.shape
    f = w_out.shape[0]
    row = lambda i: (i, 0)
    halo = lambda i: (jnp.maximum(i * (tm // HALO) - 1, 0), 0)
    final = final_g is not None
    in_specs = [pl.BlockSpec((tm, d), row), pl.BlockSpec((HALO, d), halo)]
    args = [x2, x2]
    for t in mix:
        in_specs += [pl.BlockSpec((tm, t.shape[1]), row), pl.BlockSpec((HALO, t.shape[1]), halo)]
        args += [t, t]
    in_specs += [_resident(w_o.shape), _resident((1, d)), _resident(w_in.shape),
                 _resident(conv_w.shape), _resident(conv_b.shape), _resident(w_out.shape)]
    args += [w_o, g, w_in, conv_w, conv_b, w_out]
    if final:
        in_specs.append(_resident((1, d)))
        args.append(final_g)
    return pl.pallas_call(
        functools.partial(_mixer_ffn_kernel, seq_len=seq_len, tf=tf, gated=len(mix) == 2,
                          final=final),
        out_shape=jax.ShapeDtypeStruct((m, d), F32),
        grid=(m // tm,),
        in_specs=in_specs,
        out_specs=pl.BlockSpec((tm, d), row),
        scratch_shapes=[pltpu.VMEM((tm, f), BF16)],
        compiler_params=_params("parallel"),
        name="mixer_ffn",
    )(*args)


def _forward(x, positions, norm_mix_g, norm_ffn_g, final_norm_g, attn_w_qkv, attn_w_o,
             attn_lambda_q1, attn_lambda_k1, attn_lambda_q2, attn_lambda_k2, attn_subln_g,
             ret_w_qkvg, ret_w_o, ffn_w_in, ffn_conv_w, ffn_conv_b, ffn_w_out,
             *, tm, tq, ret_chunk, tf):
    b, s, d = x.shape
    depth = norm_mix_g.shape[0]
    x2 = x.reshape(b * s, d)
    attn_tables = _attn_rope_tables(positions)
    ret_tables = _ret_rope_tables(positions)
    ret_heads = d // RET_KEY_DIM
    assert tm % ret_chunk == 0 and s % tm == 0
    proj_decay, chunk_decay = _ret_decay_tables(ret_heads, ret_chunk, tm)
    perm = _ret_column_permutation(d)

    for i in range(depth):
        j = i // N_MIXERS
        g_mix = norm_mix_g[i].reshape(1, d)
        if i % N_MIXERS == 0:
            lambda_init = 0.8 - 0.6 * math.exp(-0.3 * i)
            qt, k, vt = _attn_qkv(x2, g_mix, attn_w_qkv[j].astype(BF16), attn_tables, b, s, tm)
            lam_vecs = jnp.stack([attn_lambda_q1[j], attn_lambda_k1[j],
                                  attn_lambda_q2[j], attn_lambda_k2[j]]).astype(F32)
            o = _diff_attn(lam_vecs, attn_subln_g[j].reshape(1, -1), qt, k, vt, b, s, tq, lambda_init)
            mix, w_o = (o,), attn_w_o[j]
        else:
            w = ret_w_qkvg[j]
            w = jnp.concatenate([w[:, :d][:, perm], w[:, d:2 * d][:, perm], w[:, 2 * d:]], axis=1)
            q, kt, v, gate = _ret_qkvg(x2, g_mix, w.astype(BF16), ret_tables, proj_decay, b, s, tm)
            o = _retention(chunk_decay, q, kt, v, b, s, ret_chunk)
            mix, w_o = (o, gate), ret_w_o[j]
        final_g = final_norm_g.reshape(1, d) if i == depth - 1 else None
        x2 = _mixer_ffn(x2, mix, w_o.astype(BF16), norm_ffn_g[i].reshape(1, d),
                        ffn_w_in[i].astype(BF16), ffn_conv_w[i], ffn_conv_b[i].reshape(1, -1),
                        ffn_w_out[i].astype(BF16), final_g, s, tm, tf)
    return x2.reshape(b, s, d)


def kernel(x, positions, norm_mix_g, norm_ffn_g, final_norm_g, attn_w_qkv, attn_w_o,
           attn_lambda_q1, attn_lambda_k1, attn_lambda_q2, attn_lambda_k2, attn_subln_g,
           ret_w_qkvg, ret_w_o, ffn_w_in, ffn_conv_w, ffn_conv_b, ffn_w_out):
    return _forward(x, positions, norm_mix_g, norm_ffn_g, final_norm_g, attn_w_qkv, attn_w_o,
                    attn_lambda_q1, attn_lambda_k1, attn_lambda_q2, attn_lambda_k2, attn_subln_g,
                    ret_w_qkvg, ret_w_o, ffn_w_in, ffn_conv_w, ffn_conv_b, ffn_w_out,
                    tm=512, tq=256, ret_chunk=256, tf=256)
```
